```python
import jax, jax.numpy as jnp
from jax import lax
import numpy as np

D_MODEL = 2048
BATCH = 1
SEQ = 8192
DEPTH = 4

CHUNK = 64
D_SSM = D_MODEL
SSM_HEAD_DIM = 64
SSM_HEADS = D_SSM // SSM_HEAD_DIM
SSM_GROUPS = 8
HEADS_PER_GROUP = SSM_HEADS // SSM_GROUPS
D_STATE = 128
SSM_CONV = 4
D_XBC = D_SSM + 2 * SSM_GROUPS * D_STATE
NORM_GROUPS = 8
D_CONV = D_MODEL
CONV_GROUPS = 32
CONFORMER_KERNEL = 31
D_MIX = D_SSM + D_CONV
SPLITS = [D_SSM, D_SSM + D_XBC, D_SSM + D_XBC + SSM_HEADS,
          D_SSM + D_XBC + SSM_HEADS + D_CONV, D_SSM + D_XBC + SSM_HEADS + 2 * D_CONV]
D_IN = D_SSM + D_XBC + SSM_HEADS + 3 * D_CONV
EPS = 1e-5

kernel_name = "hybrid_ssd_conformer_conv_trunk"


def rmsnorm(x, w):
    xf = x.astype(jnp.float32)
    y = xf * lax.rsqrt(jnp.mean(xf * xf, axis=-1, keepdims=True) + EPS)
    return (y * w.astype(jnp.float32)).astype(x.dtype)


def causal_depthwise_conv(x, w, b):
    k, c = w.shape
    y = lax.conv_general_dilated(x, w[:, None, :].astype(x.dtype), window_strides=(1,),
                                 padding=[(k - 1, 0)], dimension_numbers=('NWC', 'WIO', 'NWC'),
                                 feature_group_count=c)
    return y + b.astype(x.dtype)


def ssd_chunked(x, dt, a, bmat, cmat):
    b, l, h, p = x.shape
    g, n = bmat.shape[-2:]
    r = h // g
    c = l // CHUNK
    q = CHUNK
    xf = x.astype(jnp.float32)
    xdt = (xf * dt[..., None]).reshape(b, c, q, g, r, p)
    adt = jnp.moveaxis((dt * a).reshape(b, c, q, g, r), 2, -1)
    a_cs = jnp.cumsum(adt, axis=-1)
    bc = bmat.astype(jnp.float32).reshape(b, c, q, g, n)
    cc = cmat.astype(jnp.float32).reshape(b, c, q, g, n)
    causal = jnp.tril(jnp.ones((q, q), dtype=bool))
    seg = a_cs[..., :, None] - a_cs[..., None, :]
    decay = jnp.exp(jnp.where(causal, seg, -jnp.inf))
    cb = jnp.einsum('bclgn,bcsgn->bcgls', cc, bc)
    y_diag = jnp.einsum('bcgls,bcgrls,bcsgrp->bclgrp', cb, decay, xdt)
    decay_states = jnp.exp(a_cs[..., -1:] - a_cs)
    states = jnp.einsum('bcsgn,bcgrs,bcsgrp->bcgrpn', bc, decay_states, xdt)
    chunk_decay = jnp.exp(a_cs[..., -1])

    def step(carry, inp):
        st, dec = inp
        return carry * dec[..., None, None] + st, carry

    init = jnp.zeros((b, g, r, p, n), jnp.float32)
    _, prev = lax.scan(step, init, (jnp.moveaxis(states, 1, 0), jnp.moveaxis(chunk_decay, 1, 0)))
    prev = jnp.moveaxis(prev, 0, 1)
    y_off = jnp.einsum('bclgn,bcgrpn,bcgrl->bclgrp', cc, prev, jnp.exp(a_cs))
    return (y_diag + y_off).reshape(b, l, h, p)


def hybrid_layer(h, norm_w, w_in, ssm_conv_w, ssm_conv_b, dt_bias, a_log, d_skip, ssm_norm_w,
                 conf_conv_w, conf_conv_b, ln_w, ln_b, w_out):
    b, l, _ = h.shape
    u = rmsnorm(h, norm_w)
    proj = jnp.einsum('bld,de->ble', u, w_in.astype(u.dtype))
    z, xbc, dt_raw, cv, ca, cg = jnp.split(proj, SPLITS, axis=-1)

    xbc = jax.nn.silu(causal_depthwise_conv(xbc, ssm_conv_w, ssm_conv_b))
    xs, bm, cm = jnp.split(xbc, [D_SSM, D_SSM + SSM_GROUPS * D_STATE], axis=-1)
    dt = jax.nn.softplus(dt_raw.astype(jnp.float32) + dt_bias.astype(jnp.float32))
    a = -jnp.exp(a_log.astype(jnp.float32))
    xh = xs.reshape(b, l, SSM_HEADS, SSM_HEAD_DIM)
    y = ssd_chunked(xh, dt, a, bm.reshape(b, l, SSM_GROUPS, D_STATE),
                    cm.reshape(b, l, SSM_GROUPS, D_STATE))
    y = y + d_skip.astype(jnp.float32)[:, None] * xh.astype(jnp.float32)
    yz = (y.reshape(b, l, D_SSM) * jax.nn.silu(z.astype(jnp.float32))).reshape(b, l, NORM_GROUPS, -1)
    yz = yz * lax.rsqrt(jnp.mean(yz * yz, axis=-1, keepdims=True) + EPS)
    y_ssd = yz.reshape(b, l, D_SSM) * ssm_norm_w.astype(jnp.float32)

    glu = cv * jax.nn.sigmoid(ca)
    cc = causal_depthwise_conv(glu, conf_conv_w, conf_conv_b).astype(jnp.float32)
    mu = jnp.mean(cc, axis=-1, keepdims=True)
    var = jnp.mean(jnp.square(cc - mu), axis=-1, keepdims=True)
    cc = (cc - mu) * lax.rsqrt(var + EPS) * ln_w.astype(jnp.float32) + ln_b.astype(jnp.float32)
    y_conv = jax.nn.silu(cc) * jax.nn.silu(cg.astype(jnp.float32))

    mixed = jnp.concatenate([y_ssd, y_conv], axis=-1).astype(h.dtype)
    return h + jnp.einsum('ble,ed->bld', mixed, w_out.astype(h.dtype))


def setup_inputs(seed: int = 0) -> dict:
    key = jax.random.key(seed)
    ks = jax.random.split(key, 16)
    f32 = jnp.float32
    x = jax.random.normal(ks[0], (BATCH, SEQ, D_MODEL), f32)
    norm_w = 1.0 + 0.02 * jax.random.normal(ks[1], (DEPTH, D_MODEL), f32)
    w_in = jax.random.normal(ks[2], (DEPTH, D_MODEL, D_IN), f32) * D_MODEL ** -0.5
    ssm_conv_w = jax.random.normal(ks[3], (DEPTH, SSM_CONV, D_XBC), f32) * SSM_CONV ** -0.5
    ssm_conv_b = 0.02 * jax.random.normal(ks[4], (DEPTH, D_XBC), f32)
    dt0 = jnp.exp(jax.random.uniform(ks[5], (DEPTH, SSM_HEADS), f32,
                                     minval=float(np.log(1e-3)), maxval=float(np.log(1e-1))))
    dt_bias = dt0 + jnp.log(-jnp.expm1(-dt0))
    a_log = jnp.log(jax.random.uniform(ks[6], (DEPTH, SSM_HEADS), f32, minval=1.0, maxval=16.0))
    d_skip = 1.0 + 0.02 * jax.random.normal(ks[7], (DEPTH, SSM_HEADS), f32)
    ssm_norm_w = 1.0 + 0.02 * jax.random.normal(ks[8], (DEPTH, D_SSM), f32)
    conf_conv_w = jax.random.normal(ks[9], (DEPTH, CONFORMER_KERNEL, D_CONV), f32) * CONFORMER_KERNEL ** -0.5
    conf_conv_b = 0.02 * jax.random.normal(ks[10], (DEPTH, D_CONV), f32)
    ln_w = 1.0 + 0.02 * jax.random.normal(ks[11], (DEPTH, D_CONV), f32)
    ln_b = 0.02 * jax.random.normal(ks[12], (DEPTH, D_CONV), f32)
    w_out = jax.random.normal(ks[13], (DEPTH, D_MIX, D_MODEL), f32) * D_MIX ** -0.5
    final_norm_w = 1.0 + 0.02 * jax.random.normal(ks[14], (D_MODEL,), f32)
    return {"x": x, "norm_w": norm_w, "w_in": w_in, "ssm_conv_w": ssm_conv_w,
            "ssm_conv_b": ssm_conv_b, "dt_bias": dt_bias, "a_log": a_log, "d_skip": d_skip,
            "ssm_norm_w": ssm_norm_w, "conf_conv_w": conf_conv_w, "conf_conv_b": conf_conv_b,
            "ln_w": ln_w, "ln_b": ln_b, "w_out": w_out, "final_norm_w": final_norm_w}


def reference(x, norm_w, w_in, ssm_conv_w, ssm_conv_b, dt_bias, a_log, d_skip, ssm_norm_w,
              conf_conv_w, conf_conv_b, ln_w, ln_b, w_out, final_norm_w):
    h = x
    for i in range(DEPTH):
        h = hybrid_layer(h, norm_w[i], w_in[i], ssm_conv_w[i], ssm_conv_b[i], dt_bias[i], a_log[i],
                         d_skip[i], ssm_norm_w[i], conf_conv_w[i], conf_conv_b[i], ln_w[i], ln_b[i],
                         w_out[i])
    return rmsnorm(h, final_norm_w)
```

```python
import functools

import jax
import jax.numpy as jnp
from jax import lax
from jax.experimental import pallas as pl
from jax.experimental.pallas import tpu as pltpu

D_MODEL = 2048
DEPTH = 4
CHUNK = 64
D_SSM = 2048
SSM_HEAD_DIM = 64
SSM_HEADS = 32
SSM_GROUPS = 8
HEADS_PER_GROUP = 4
D_STATE = 128
SSM_CONV = 4
D_XBC = 4096
D_CONV = 2048
CONFORMER_KERNEL = 31
EPS = 1e-5

GROUP_W = HEADS_PER_GROUP * SSM_HEAD_DIM
SSD_COLS = 2 * GROUP_W + 2 * D_STATE
CONV_BLOCK = 256
N_CONV_BLOCKS = D_CONV // CONV_BLOCK
CONV_COLS = 3 * CONV_BLOCK
LANES = 128
SUBLANES = 8
CONV_HALO = 32
N_SPLIT = 3
VMEM_LIMIT = 48 * 1024 * 1024

_f32 = jnp.float32
_bf16 = jnp.bfloat16


def _dot(a, b):
    return jnp.dot(a, b, preferred_element_type=_f32)


def _dot_nt(a, b):
    return lax.dot_general(a, b, (((1,), (1,)), ((), ())), preferred_element_type=_f32)


def _dot_tn(a, b):
    return lax.dot_general(a, b, (((0,), (0,)), ((), ())), preferred_element_type=_f32)


def _sigmoid(x):
    return 1.0 / (1.0 + jnp.exp(-x))


def _silu(x):
    return x * _sigmoid(x)


def _softplus(x):
    return jnp.maximum(x, 0.0) + jnp.log1p(jnp.exp(-jnp.abs(x)))


def _rmsnorm_rows(x, w):
    ms = jnp.mean(x * x, axis=-1, keepdims=True)
    return x * lax.rsqrt(ms + EPS) * w


def _split_bf16(v):
    pieces = []
    r = v
    for _ in range(N_SPLIT):
        p = r.astype(_bf16)
        pieces.append(p)
        r = r - p.astype(_f32)
    return jnp.concatenate(pieces, axis=-1)


def _chunk_cumsum(v):
    pos = lax.broadcasted_iota(jnp.int32, v.shape, 0) & (CHUNK - 1)
    sh = 1
    while sh < CHUNK:
        shifted = pltpu.roll(v, sh, axis=0)
        v = v + jnp.where(pos >= sh, shifted, 0.0)
        sh *= 2
    return v


def _ssd_kernel(h_ref, nw_ref, wdt_ref, dtb_ref, alog_ref, e_ref, w_ref, cw_ref, cb_ref, dsk_ref, gnw_ref,
                y_ref,
                u_scr, pa_scr, pdt_scr, tail_scr, st_scr, ext_scr, z_scr, xs_scr, bc_scr, ae_scr, dte_scr):
    i = pl.program_id(0)
    g = pl.program_id(1)
    tile = h_ref.shape[0]
    n_chunks = tile // CHUNK

    @pl.when(g == 0)
    def _():
        ub = _rmsnorm_rows(h_ref[...], nw_ref[...]).astype(_bf16)
        u_scr[...] = ub
        dt = _softplus(_dot(ub, wdt_ref[...]) + dtb_ref[...])
        a = -jnp.exp(alog_ref[...])
        pa_scr[...] = _split_bf16(_chunk_cumsum(dt * a))
        pdt_scr[...] = _split_bf16(dt)

    @pl.when(i == 0)
    def _():
        tail_scr[g] = jnp.zeros(tail_scr.shape[1:], _f32)
        st_scr[g] = jnp.zeros(st_scr.shape[1:], _f32)

    pg = _dot(u_scr[...], w_ref[...])
    z_scr[...] = pg[:, :GROUP_W]
    xbc_pre = pg[:, GROUP_W:]
    ext_scr[0:SUBLANES, :] = tail_scr[g]
    ext_scr[SUBLANES:, :] = xbc_pre
    tail_scr[g] = xbc_pre[tile - SUBLANES:, :]
    acc = cb_ref[...] + cw_ref[0:1, :] * ext_scr[pl.ds(SUBLANES - SSM_CONV + 1, tile), :]
    for k in range(1, SSM_CONV):
        acc = acc + cw_ref[k:k + 1, :] * ext_scr[pl.ds(SUBLANES - SSM_CONV + 1 + k, tile), :]
    xbc = _silu(acc)
    xs_scr[...] = xbc[:, :GROUP_W]
    bc_scr[...] = xbc[:, GROUP_W:]
    ae_scr[...] = _dot(pa_scr[...], e_ref[0])
    dte_scr[...] = _dot(pdt_scr[...], e_ref[0])

    row = lax.broadcasted_iota(jnp.int32, (CHUNK, GROUP_W), 0)
    lane_pos = lax.broadcasted_iota(jnp.int32, (CHUNK, GROUP_W), 1) & (CHUNK - 1)
    diag_mask = row == lane_pos
    causal_mask = row >= lane_pos
    bd_row = lax.broadcasted_iota(jnp.int32, (GROUP_W, GROUP_W), 0) // SSM_HEAD_DIM
    bd_col = lax.broadcasted_iota(jnp.int32, (GROUP_W, GROUP_W), 1) // SSM_HEAD_DIM
    blockdiag_mask = bd_row == bd_col
    dsk = dsk_ref[...]
    gnw = gnw_ref[...]

    def chunk_body(c, carry):
        r0 = pl.multiple_of(c * CHUNK, CHUNK)
        x_c = xs_scr[pl.ds(r0, CHUNK), :]
        b_c = bc_scr[pl.ds(r0, CHUNK), 0:D_STATE].astype(_bf16)
        c_c = bc_scr[pl.ds(r0, CHUNK), D_STATE:2 * D_STATE].astype(_bf16)
        a_c = ae_scr[pl.ds(r0, CHUNK), :]
        dt_c = dte_scr[pl.ds(r0, CHUNK), :]
        z_c = z_scr[pl.ds(r0, CHUNK), :]
        state = st_scr[g]
        a_last = a_c[CHUNK - 1:CHUNK, :]
        xdt = x_c * dt_c
        y_off = _dot(c_c, state.astype(_bf16)) * jnp.exp(a_c)
        cb4 = _dot_nt(c_c, jnp.concatenate([b_c] * HEADS_PER_GROUP, axis=0))
        a_row = jnp.sum(jnp.where(diag_mask, a_c, 0.0), axis=0, keepdims=True)
        decay = jnp.where(causal_mask, jnp.exp(a_c - a_row), 0.0)
        scores = (cb4 * decay).astype(_bf16)
        x_bd = jnp.where(blockdiag_mask, jnp.concatenate([xdt] * HEADS_PER_GROUP, axis=0), 0.0).astype(_bf16)
        y_diag = _dot(scores, x_bd)
        xw = (xdt * jnp.exp(a_last - a_c)).astype(_bf16)
        st_scr[g] = state * jnp.exp(a_last) + _dot_tn(b_c, xw)
        y = y_diag + y_off + dsk * x_c
        yz = y * _silu(z_c)
        out = _rmsnorm_rows(yz, gnw)
        y_ref[pl.ds(r0, CHUNK), :] = out.astype(y_ref.dtype)
        return carry

    lax.fori_loop(0, n_chunks, chunk_body, 0)


def _ssd_branch(h, nw, wdt, dtb, alog, e_all, w_ssd, cw, cb, dsk, gnw, tile):
    seq = h.shape[0]
    n_tiles = seq // tile
    const2 = lambda i, g: (0, 0)
    return pl.pallas_call(
        _ssd_kernel,
        out_shape=jax.ShapeDtypeStruct((seq, D_SSM), _bf16),
        grid=(n_tiles, SSM_GROUPS),
        in_specs=[
            pl.BlockSpec((tile, D_MODEL), lambda i, g: (i, 0)),
            pl.BlockSpec((1, D_MODEL), const2),
            pl.BlockSpec((D_MODEL, LANES), const2),
            pl.BlockSpec((1, LANES), const2),
            pl.BlockSpec((1, LANES), const2),
            pl.BlockSpec((1, N_SPLIT * LANES, GROUP_W), lambda i, g: (g, 0, 0)),
            pl.BlockSpec((D_MODEL, SSD_COLS), lambda i, g: (0, g)),
            pl.BlockSpec((SSM_CONV, 2 * GROUP_W), lambda i, g: (0, g)),
            pl.BlockSpec((1, 2 * GROUP_W), lambda i, g: (0, g)),
            pl.BlockSpec((1, GROUP_W), lambda i, g: (0, g)),
            pl.BlockSpec((1, GROUP_W), lambda i, g: (0, g)),
        ],
        out_specs=pl.BlockSpec((tile, GROUP_W), lambda i, g: (i, g)),
        scratch_shapes=[
            pltpu.VMEM((tile, D_MODEL), _bf16),
            pltpu.VMEM((tile, N_SPLIT * LANES), _bf16),
            pltpu.VMEM((tile, N_SPLIT * LANES), _bf16),
            pltpu.VMEM((SSM_GROUPS, SUBLANES, 2 * GROUP_W), _f32),
            pltpu.VMEM((SSM_GROUPS, D_STATE, GROUP_W), _f32),
            pltpu.VMEM((tile + SUBLANES, 2 * GROUP_W), _f32),
            pltpu.VMEM((tile, GROUP_W), _f32),
            pltpu.VMEM((tile, GROUP_W), _f32),
            pltpu.VMEM((tile, 2 * D_STATE), _f32),
            pltpu.VMEM((tile, GROUP_W), _f32),
            pltpu.VMEM((tile, GROUP_W), _f32),
        ],
        compiler_params=pltpu.CompilerParams(
            dimension_semantics=("arbitrary", "arbitrary"), vmem_limit_bytes=VMEM_LIMIT),
        name="ssd_branch",
    )(h, nw, wdt, dtb, alog, e_all, w_ssd, cw, cb, dsk, gnw)


def _conv_kernel(h_ref, nw_ref, w_ref, kw_ref, kb_ref, lnw_ref, lnb_ref,
                 y_ref,
                 u_scr, tail_scr, ext_scr, cc_scr, sg_scr):
    i = pl.program_id(0)
    j = pl.program_id(1)
    tile = h_ref.shape[0]

    @pl.when(j == 0)
    def _():
        u_scr[...] = _rmsnorm_rows(h_ref[...], nw_ref[...]).astype(_bf16)

    @pl.when(i == 0)
    def _():
        tail_scr[j] = jnp.zeros(tail_scr.shape[1:], _f32)

    pc = _dot(u_scr[...], w_ref[...])
    glu = pc[:, :CONV_BLOCK] * _sigmoid(pc[:, CONV_BLOCK:2 * CONV_BLOCK])
    sg_scr[j] = _silu(pc[:, 2 * CONV_BLOCK:])
    ext_scr[0:CONV_HALO, :] = tail_scr[j]
    ext_scr[CONV_HALO:, :] = glu
    tail_scr[j] = glu[tile - CONV_HALO:, :]
    first = CONV_HALO - CONFORMER_KERNEL + 1
    acc = kb_ref[...] + kw_ref[0:1, :] * ext_scr[pl.ds(first, tile), :]
    for k in range(1, CONFORMER_KERNEL):
        acc = acc + kw_ref[k:k + 1, :] * ext_scr[pl.ds(first + k, tile), :]
    cc_scr[j] = acc

    @pl.when(j == N_CONV_BLOCKS - 1)
    def _():
        s1 = jnp.zeros((tile, 1), _f32)
        for jj in range(N_CONV_BLOCKS):
            s1 = s1 + jnp.sum(cc_scr[jj], axis=-1, keepdims=True)
        mu = s1 * (1.0 / D_CONV)
        s2 = jnp.zeros((tile, 1), _f32)
        for jj in range(N_CONV_BLOCKS):
            d = cc_scr[jj] - mu
            s2 = s2 + jnp.sum(d * d, axis=-1, keepdims=True)
        rstd = lax.rsqrt(s2 * (1.0 / D_CONV) + EPS)
        for jj in range(N_CONV_BLOCKS):
            sl = slice(jj * CONV_BLOCK, (jj + 1) * CONV_BLOCK)
            v = (cc_scr[jj] - mu) * rstd * lnw_ref[:, sl] + lnb_ref[:, sl]
            y_ref[:, sl] = (_silu(v) * sg_scr[jj]).astype(y_ref.dtype)


def _conv_branch(h, nw, w_conv, kw, kb, lnw, lnb, tile):
    seq = h.shape[0]
    n_tiles = seq // tile
    const2 = lambda i, j: (0, 0)
    return pl.pallas_call(
        _conv_kernel,
        out_shape=jax.ShapeDtypeStruct((seq, D_CONV), _bf16),
        grid=(n_tiles, N_CONV_BLOCKS),
        in_specs=[
            pl.BlockSpec((tile, D_MODEL), lambda i, j: (i, 0)),
            pl.BlockSpec((1, D_MODEL), const2),
            pl.BlockSpec((D_MODEL, CONV_COLS), lambda i, j: (0, j)),
            pl.BlockSpec((CONFORMER_KERNEL, CONV_BLOCK), lambda i, j: (0, j)),
            pl.BlockSpec((1, CONV_BLOCK), lambda i, j: (0, j)),
            pl.BlockSpec((1, D_CONV), const2),
            pl.BlockSpec((1, D_CONV), const2),
        ],
        out_specs=pl.BlockSpec((tile, D_CONV), lambda i, j: (i, 0)),
        scratch_shapes=[
            pltpu.VMEM((tile, D_MODEL), _bf16),
            pltpu.VMEM((N_CONV_BLOCKS, CONV_HALO, CONV_BLOCK), _f32),
            pltpu.VMEM((tile + CONV_HALO, CONV_BLOCK), _f32),
            pltpu.VMEM((N_CONV_BLOCKS, tile, CONV_BLOCK), _f32),
            pltpu.VMEM((N_CONV_BLOCKS, tile, CONV_BLOCK), _f32),
        ],
        compiler_params=pltpu.CompilerParams(
            dimension_semantics=("arbitrary", "arbitrary"), vmem_limit_bytes=VMEM_LIMIT),
        name="conv_branch",
    )(h, nw, w_conv, kw, kb, lnw, lnb)


def _out_kernel(ys_ref, yc_ref, w1_ref, w2_ref, h_ref, o_ref):
    o_ref[...] = h_ref[...] + _dot(ys_ref[...], w1_ref[...]) + _dot(yc_ref[...], w2_ref[...])


def _out_proj(ys, yc, w1, w2, h, tm, tn):
    seq = h.shape[0]
    return pl.pallas_call(
        _out_kernel,
        out_shape=jax.ShapeDtypeStruct((seq, D_MODEL), _f32),
        grid=(D_MODEL // tn, seq // tm),
        in_specs=[
            pl.BlockSpec((tm, D_SSM), lambda n, m: (m, 0)),
            pl.BlockSpec((tm, D_CONV), lambda n, m: (m, 0)),
            pl.BlockSpec((D_SSM, tn), lambda n, m: (0, n)),
            pl.BlockSpec((D_CONV, tn), lambda n, m: (0, n)),
            pl.BlockSpec((tm, tn), lambda n, m: (m, n)),
        ],
        out_specs=pl.BlockSpec((tm, tn), lambda n, m: (m, n)),
        compiler_params=pltpu.CompilerParams(
            dimension_semantics=("arbitrary", "arbitrary"), vmem_limit_bytes=VMEM_LIMIT),
        name="out_proj",
    )(ys, yc, w1, w2, h)


def _final_norm_kernel(h_ref, w_ref, o_ref):
    o_ref[...] = _rmsnorm_rows(h_ref[...], w_ref[...])


def _final_norm(h, w, tile):
    seq = h.shape[0]
    return pl.pallas_call(
        _final_norm_kernel,
        out_shape=jax.ShapeDtypeStruct((seq, D_MODEL), _f32),
        grid=(seq // tile,),
        in_specs=[pl.BlockSpec((tile, D_MODEL), lambda i: (i, 0)),
                  pl.BlockSpec((1, D_MODEL), lambda i: (0, 0))],
        out_specs=pl.BlockSpec((tile, D_MODEL), lambda i: (i, 0)),
        compiler_params=pltpu.CompilerParams(
            dimension_semantics=("arbitrary",), vmem_limit_bytes=VMEM_LIMIT),
        name="final_norm",
    )(h, w)


def _pad_lanes(v):
    return jnp.pad(v, [(0, 0)] * (v.ndim - 1) + [(0, LANES - v.shape[-1])])


def _group_columns(parts, n_groups):
    lead = parts[0].shape[:-1]
    pieces = [p.reshape(lead + (n_groups, p.shape[-1] // n_groups)) for p in parts]
    out = jnp.concatenate(pieces, axis=-1)
    return out.reshape(lead + (n_groups * out.shape[-1],))


def _expansion_matrix():
    head_of_row = jnp.arange(N_SPLIT * LANES) % LANES
    g = jnp.arange(SSM_GROUPS)[:, None, None]
    head_of_col = g * HEADS_PER_GROUP + jnp.arange(GROUP_W)[None, None, :] // SSM_HEAD_DIM
    return (head_of_row[None, :, None] == head_of_col).astype(_bf16)


def _pick_tile(seq, want):
    t = min(seq, want)
    assert seq % t == 0 and t % CHUNK == 0
    return t


def kernel(x, norm_w, w_in, ssm_conv_w, ssm_conv_b, dt_bias, a_log, d_skip, ssm_norm_w, conf_conv_w,
           conf_conv_b, ln_w, ln_b, w_out, final_norm_w):
    batch, seq, _ = x.shape
    assert batch == 1
    tile = _pick_tile(seq, 512)

    o_x = D_SSM
    o_b = o_x + D_SSM
    o_c = o_b + SSM_GROUPS * D_STATE
    o_dt = o_c + SSM_GROUPS * D_STATE
    o_cv = o_dt + SSM_HEADS
    o_ca = o_cv + D_CONV
    o_cg = o_ca + D_CONV
    wb = w_in.astype(_bf16)
    w_ssd = _group_columns([wb[..., 0:o_x], wb[..., o_x:o_b], wb[..., o_b:o_c], wb[..., o_c:o_dt]], SSM_GROUPS)
    w_dt = _pad_lanes(wb[..., o_dt:o_cv])
    w_conv = _group_columns([wb[..., o_cv:o_ca], wb[..., o_ca:o_cg], wb[..., o_cg:]], N_CONV_BLOCKS)
    wo = w_out.astype(_bf16)
    cw = _group_columns([ssm_conv_w[..., 0:D_SSM], ssm_conv_w[..., D_SSM:D_SSM + SSM_GROUPS * D_STATE],
                         ssm_conv_w[..., D_SSM + SSM_GROUPS * D_STATE:]], SSM_GROUPS)
    cb = _group_columns([ssm_conv_b[..., 0:D_SSM], ssm_conv_b[..., D_SSM:D_SSM + SSM_GROUPS * D_STATE],
                         ssm_conv_b[..., D_SSM + SSM_GROUPS * D_STATE:]], SSM_GROUPS)
    dsk = jnp.repeat(d_skip, SSM_HEAD_DIM, axis=-1)
    e_all = _expansion_matrix()

    h = x[0]
    for l in range(DEPTH):
        ys = _ssd_branch(h, norm_w[l][None], w_dt[l], _pad_lanes(dt_bias[l][None]), _pad_lanes(a_log[l][None]),
                         e_all, w_ssd[l], cw[l], cb[l][None], dsk[l][None], ssm_norm_w[l][None], tile)
        yc = _conv_branch(h, norm_w[l][None], w_conv[l], conf_conv_w[l], conf_conv_b[l][None],
                          ln_w[l][None], ln_b[l][None], tile)
        h = _out_proj(ys, yc, wo[l, :D_SSM], wo[l, D_SSM:], h, tile, 1024)
    return _final_norm(h, final_norm_w[None], tile)[None]
```

```python
import jax
import jax.numpy as jnp
from jax import lax
from jax.experimental import pallas as pl
from jax.experimental.pallas import tpu as pltpu

D_MODEL = 2048
DEPTH = 4
CHUNK = 64
D_SSM = 2048
SSM_HEAD_DIM = 64
SSM_HEADS = 32
SSM_GROUPS = 8
HEADS_PER_GROUP = 4
D_STATE = 128
SSM_CONV = 4
D_CONV = 2048
CONFORMER_KERNEL = 31
EPS = 1e-5

GROUP_W = HEADS_PER_GROUP * SSM_HEAD_DIM
SSD_COLS = 2 * GROUP_W + 2 * D_STATE
CONV_BLOCK = 256
N_CONV_BLOCKS = D_CONV // CONV_BLOCK
CONV_COLS = 3 * CONV_BLOCK
LANES = 128
SUBLANES = 8
CONV_HALO = 32
N_SPLIT = 3
VMEM_LIMIT = 48 * 1024 * 1024

OFF_X = D_SSM
OFF_B = OFF_X + D_SSM
OFF_C = OFF_B + SSM_GROUPS * D_STATE
OFF_DT = OFF_C + SSM_GROUPS * D_STATE
OFF_CV = OFF_DT + SSM_HEADS
OFF_CA = OFF_CV + D_CONV
OFF_CG = OFF_CA + D_CONV
CONV_LANE_SHIFT = OFF_CV % LANES

_f32 = jnp.float32
_bf16 = jnp.bfloat16


def _dot(a, b):
    return jnp.dot(a, b, preferred_element_type=_f32)


def _dot_nt(a, b):
    return lax.dot_general(a, b, (((1,), (1,)), ((), ())), preferred_element_type=_f32)


def _dot_tn(a, b):
    return lax.dot_general(a, b, (((0,), (0,)), ((), ())), preferred_element_type=_f32)


def _sigmoid(x):
    return 1.0 / (1.0 + jnp.exp(-x))


def _silu(x):
    return x * _sigmoid(x)


def _softplus(x):
    return jnp.maximum(x, 0.0) + jnp.log1p(jnp.exp(-jnp.abs(x)))


def _rmsnorm_rows(x, w):
    ms = jnp.mean(x * x, axis=-1, keepdims=True)
    return x * lax.rsqrt(ms + EPS) * w


def _split_bf16(v):
    pieces = []
    r = v
    for _ in range(N_SPLIT):
        p = r.astype(_bf16)
        pieces.append(p)
        r = r - p.astype(_f32)
    return jnp.concatenate(pieces, axis=-1)


def _chunk_cumsum(v):
    pos = lax.broadcasted_iota(jnp.int32, v.shape, 0) & (CHUNK - 1)
    sh = 1
    while sh < CHUNK:
        shifted = pltpu.roll(v, sh, axis=0)
        v = v + jnp.where(pos >= sh, shifted, 0.0)
        sh *= 2
    return v


def _w_layout_kernel(z_ref, x_ref, b_ref, c_ref, dt_ref, cv0_ref, cv1_ref, ca0_ref, ca1_ref, cg0_ref, cg1_ref,
                     ssd_ref, dtw_ref, conv_ref):
    ssd_ref[...] = jnp.concatenate([z_ref[...], x_ref[...], b_ref[...], c_ref[...]], axis=-1).astype(_bf16)
    dtw_ref[...] = dt_ref[...].astype(_bf16)

    def shifted(r0, r1):
        wide = jnp.concatenate([r0[...], r1[...]], axis=-1)
        return wide[:, CONV_LANE_SHIFT:CONV_LANE_SHIFT + CONV_BLOCK]

    conv_ref[...] = jnp.concatenate(
        [shifted(cv0_ref, cv1_ref), shifted(ca0_ref, ca1_ref), shifted(cg0_ref, cg1_ref)], axis=-1).astype(_bf16)


def _w_layout(w_in, rows):
    assert OFF_CV // LANES == OFF_DT // LANES and SSM_GROUPS == N_CONV_BLOCKS

    def wide(off):
        first = (off - CONV_LANE_SHIFT) // CONV_BLOCK
        return pl.BlockSpec((None, rows, CONV_BLOCK), lambda l, r, j: (l, r, first + j))

    def narrow(off):
        first = (off - CONV_LANE_SHIFT + CONV_BLOCK) // LANES
        return pl.BlockSpec((None, rows, LANES), lambda l, r, j: (l, r, first + 2 * j))

    in_specs = [
        pl.BlockSpec((None, rows, GROUP_W), lambda l, r, j: (l, r, j)),
        pl.BlockSpec((None, rows, GROUP_W), lambda l, r, j: (l, r, OFF_X // GROUP_W + j)),
        pl.BlockSpec((None, rows, D_STATE), lambda l, r, j: (l, r, OFF_B // D_STATE + j)),
        pl.BlockSpec((None, rows, D_STATE), lambda l, r, j: (l, r, OFF_C // D_STATE + j)),
        pl.BlockSpec((None, rows, LANES), lambda l, r, j: (l, r, OFF_DT // LANES)),
        wide(OFF_CV), narrow(OFF_CV), wide(OFF_CA), narrow(OFF_CA), wide(OFF_CG), narrow(OFF_CG),
    ]
    return pl.pallas_call(
        _w_layout_kernel,
        out_shape=(jax.ShapeDtypeStruct((DEPTH, D_MODEL, SSM_GROUPS * SSD_COLS), _bf16),
                   jax.ShapeDtypeStruct((DEPTH, D_MODEL, LANES), _bf16),
                   jax.ShapeDtypeStruct((DEPTH, D_MODEL, N_CONV_BLOCKS * CONV_COLS), _bf16)),
        grid=(DEPTH, D_MODEL // rows, SSM_GROUPS),
        in_specs=in_specs,
        out_specs=(pl.BlockSpec((None, rows, SSD_COLS), lambda l, r, j: (l, r, j)),
                   pl.BlockSpec((None, rows, LANES), lambda l, r, j: (l, r, 0)),
                   pl.BlockSpec((None, rows, CONV_COLS), lambda l, r, j: (l, r, j))),
        compiler_params=pltpu.CompilerParams(
            dimension_semantics=("arbitrary", "arbitrary", "arbitrary"), vmem_limit_bytes=VMEM_LIMIT),
        name="w_layout",
    )(*([w_in] * len(in_specs)))


def _ssd_mix(src, grp, e_ref, cw_ref, cb_ref, dsk_ref, gnw_ref, y_ref, pa_scr, pdt_scr, tail_scr, st_scr, ext_scr):
    tile = src.shape[0]
    n_chunks = tile // CHUNK
    ext_scr[0:SUBLANES, :] = tail_scr[grp]
    ext_scr[SUBLANES:, :] = src[:, GROUP_W:]
    tail_scr[grp] = src[tile - SUBLANES:, GROUP_W:]
    acc = cb_ref[...] + cw_ref[0:1, :] * ext_scr[pl.ds(SUBLANES - SSM_CONV + 1, tile), :]
    for k in range(1, SSM_CONV):
        acc = acc + cw_ref[k:k + 1, :] * ext_scr[pl.ds(SUBLANES - SSM_CONV + 1 + k, tile), :]
    xbc = _silu(acc)
    xs = xbc[:, :GROUP_W]
    bm = xbc[:, GROUP_W:GROUP_W + D_STATE].astype(_bf16)
    cm = xbc[:, GROUP_W + D_STATE:].astype(_bf16)
    a_e = _dot(pa_scr[...], e_ref[...])
    dt_e = _dot(pdt_scr[...], e_ref[...])

    row = lax.broadcasted_iota(jnp.int32, (CHUNK, GROUP_W), 0)
    lane_pos = lax.broadcasted_iota(jnp.int32, (CHUNK, GROUP_W), 1) & (CHUNK - 1)
    diag_mask = row == lane_pos
    causal_mask = row >= lane_pos
    bd_row = lax.broadcasted_iota(jnp.int32, (GROUP_W, GROUP_W), 0) // SSM_HEAD_DIM
    bd_col = lax.broadcasted_iota(jnp.int32, (GROUP_W, GROUP_W), 1) // SSM_HEAD_DIM
    blockdiag_mask = bd_row == bd_col
    dsk = dsk_ref[...]

    state = st_scr[grp]
    ys = []
    for c in range(n_chunks):
        rows = slice(c * CHUNK, (c + 1) * CHUNK)
        x_c, b_c, c_c, a_c = xs[rows], bm[rows], cm[rows], a_e[rows]
        a_last = a_c[CHUNK - 1:CHUNK, :]
        xdt = x_c * dt_e[rows]
        y_off = _dot(c_c, state.astype(_bf16)) * jnp.exp(a_c)
        cb4 = _dot_nt(c_c, jnp.concatenate([b_c] * HEADS_PER_GROUP, axis=0))
        a_row = jnp.sum(jnp.where(diag_mask, a_c, 0.0), axis=0, keepdims=True)
        decay = jnp.where(causal_mask, jnp.exp(a_c - a_row), 0.0)
        scores = (cb4 * decay).astype(_bf16)
        x_bd = jnp.where(blockdiag_mask, jnp.concatenate([xdt] * HEADS_PER_GROUP, axis=0), 0.0).astype(_bf16)
        y_diag = _dot(scores, x_bd)
        xw = (xdt * jnp.exp(a_last - a_c)).astype(_bf16)
        state = state * jnp.exp(a_last) + _dot_tn(b_c, xw)
        ys.append(y_diag + y_off + dsk * x_c)
    st_scr[grp] = state
    yz = jnp.concatenate(ys, axis=0) * _silu(src[:, :GROUP_W])
    y_ref[...] = _rmsnorm_rows(yz, gnw_ref[...]).astype(y_ref.dtype)


def _ssd_kernel(h_ref, nw_ref, wdt_ref, dtb_ref, alog_ref, e_ref, w_ref, cw_ref, cb_ref, dsk_ref, gnw_ref,
                y_ref,
                u_scr, pa_scr, pdt_scr, tail_scr, st_scr, ext_scr, pg_scr):
    i = pl.program_id(0)
    g = pl.program_id(1)

    @pl.when((i == 0) & (g == 0))
    def _():
        tail_scr[...] = jnp.zeros(tail_scr.shape, _f32)
        st_scr[...] = jnp.zeros(st_scr.shape, _f32)

    @pl.when(g == 0)
    def _():
        ub = _rmsnorm_rows(h_ref[...], nw_ref[...]).astype(_bf16)
        u_scr[...] = ub
        dt = _softplus(_dot(ub, wdt_ref[...]) + dtb_ref[...])
        a = -jnp.exp(alog_ref[...])
        pa_scr[...] = _split_bf16(_chunk_cumsum(dt * a))
        pdt_scr[...] = _split_bf16(dt)

    pg_scr[...] = _dot(u_scr[...], w_ref[...])
    _ssd_mix(pg_scr, g, e_ref, cw_ref, cb_ref, dsk_ref, gnw_ref, y_ref, pa_scr, pdt_scr, tail_scr, st_scr, ext_scr)


def _ssd_branch(l, h, nw, wdt, dtb, alog, e_all, w_ssd, cw, cb, dsk, gnw, tile):
    seq = h.shape[0]
    n_tiles = seq // tile
    layer_row = lambda i, g: (l, 0, 0)
    layer_group = lambda i, g: (l, 0, g)
    return pl.pallas_call(
        _ssd_kernel,
        out_shape=jax.ShapeDtypeStruct((seq, D_SSM), _bf16),
        grid=(n_tiles, SSM_GROUPS),
        in_specs=[
            pl.BlockSpec((tile, D_MODEL), lambda i, g: (i, 0)),
            pl.BlockSpec((None, 1, D_MODEL), layer_row),
            pl.BlockSpec((None, D_MODEL, LANES), layer_row),
            pl.BlockSpec((None, 1, LANES), layer_row),
            pl.BlockSpec((None, 1, LANES), layer_row),
            pl.BlockSpec((None, N_SPLIT * LANES, GROUP_W), lambda i, g: (g, 0, 0)),
            pl.BlockSpec((None, D_MODEL, SSD_COLS), layer_group),
            pl.BlockSpec((None, SSM_CONV, 2 * GROUP_W), layer_group),
            pl.BlockSpec((None, 1, 2 * GROUP_W), layer_group),
            pl.BlockSpec((None, 1, GROUP_W), layer_group),
            pl.BlockSpec((None, 1, GROUP_W), layer_group),
        ],
        out_specs=pl.BlockSpec((tile, GROUP_W), lambda i, g: (i, g)),
        scratch_shapes=[
            pltpu.VMEM((tile, D_MODEL), _bf16),
            pltpu.VMEM((tile, N_SPLIT * LANES), _bf16),
            pltpu.VMEM((tile, N_SPLIT * LANES), _bf16),
            pltpu.VMEM((SSM_GROUPS, SUBLANES, 2 * GROUP_W), _f32),
            pltpu.VMEM((SSM_GROUPS, D_STATE, GROUP_W), _f32),
            pltpu.VMEM((tile + SUBLANES, 2 * GROUP_W), _f32),
            pltpu.VMEM((tile, SSD_COLS), _f32),
        ],
        compiler_params=pltpu.CompilerParams(
            dimension_semantics=("arbitrary", "arbitrary"), vmem_limit_bytes=VMEM_LIMIT),
        name="ssd_branch",
    )(h, nw, wdt, dtb, alog, e_all, w_ssd, cw, cb, dsk, gnw)


def _conv_mix(src, blk, kw_ref, kb_ref, tail_scr, ext_scr, shift_scr, cc_scr, sg_scr):
    tile = src.shape[0]
    glu = src[:, :CONV_BLOCK] * _sigmoid(src[:, CONV_BLOCK:2 * CONV_BLOCK])
    sg_scr[blk] = _silu(src[:, 2 * CONV_BLOCK:])
    ext_scr[0:CONV_HALO, :] = tail_scr[blk]
    ext_scr[CONV_HALO:, :] = glu
    tail_scr[blk] = glu[tile - CONV_HALO:, :]
    shift_rows = shift_scr.shape[1]
    for sh in range(1, SUBLANES):
        shift_scr[sh - 1] = ext_scr[pl.ds(sh, shift_rows), :]
    first = CONV_HALO - CONFORMER_KERNEL + 1
    acc = kb_ref[...]
    for k in range(CONFORMER_KERNEL):
        major, minor = divmod(first + k, SUBLANES)
        if minor == 0:
            tap = ext_scr[pl.ds(major * SUBLANES, tile), :]
        else:
            tap = shift_scr[minor - 1, pl.ds(major * SUBLANES, tile), :]
        acc = acc + kw_ref[k:k + 1, :] * tap
    cc_scr[blk] = acc


def _conv_layernorm(lnw_ref, lnb_ref, y_ref, cc_scr, sg_scr):
    tile = y_ref.shape[0]
    s1 = jnp.zeros((tile, 1), _f32)
    for jj in range(N_CONV_BLOCKS):
        s1 = s1 + jnp.sum(cc_scr[jj], axis=-1, keepdims=True)
    mu = s1 * (1.0 / D_CONV)
    s2 = jnp.zeros((tile, 1), _f32)
    for jj in range(N_CONV_BLOCKS):
        d = cc_scr[jj] - mu
        s2 = s2 + jnp.sum(d * d, axis=-1, keepdims=True)
    rstd = lax.rsqrt(s2 * (1.0 / D_CONV) + EPS)
    for jj in range(N_CONV_BLOCKS):
        sl = slice(jj * CONV_BLOCK, (jj + 1) * CONV_BLOCK)
        v = (cc_scr[jj] - mu) * rstd * lnw_ref[:, sl] + lnb_ref[:, sl]
        y_ref[:, sl] = (_silu(v) * sg_scr[jj]).astype(y_ref.dtype)


def _conv_kernel(h_ref, nw_ref, w_ref, kw_ref, kb_ref, lnw_ref, lnb_ref,
                 y_ref,
                 u_scr, tail_scr, ext_scr, shift_scr, cc_scr, sg_scr, pc_scr):
    i = pl.program_id(0)
    j = pl.program_id(1)

    @pl.when((i == 0) & (j == 0))
    def _():
        tail_scr[...] = jnp.zeros(tail_scr.shape, _f32)

    @pl.when(j == 0)
    def _():
        u_scr[...] = _rmsnorm_rows(h_ref[...], nw_ref[...]).astype(_bf16)

    pc_scr[...] = _dot(u_scr[...], w_ref[...])
    _conv_mix(pc_scr, j, kw_ref, kb_ref, tail_scr, ext_scr, shift_scr, cc_scr, sg_scr)

    @pl.when(j == N_CONV_BLOCKS - 1)
    def _():
        _conv_layernorm(lnw_ref, lnb_ref, y_ref, cc_scr, sg_scr)


def _conv_branch(l, h, nw, w_conv, kw, kb, lnw, lnb, tile):
    seq = h.shape[0]
    n_tiles = seq // tile
    layer_row = lambda i, j: (l, 0, 0)
    layer_block = lambda i, j: (l, 0, j)
    shift_rows = tile + CONV_HALO - SUBLANES
    return pl.pallas_call(
        _conv_kernel,
        out_shape=jax.ShapeDtypeStruct((seq, D_CONV), _bf16),
        grid=(n_tiles, N_CONV_BLOCKS),
        in_specs=[
            pl.BlockSpec((tile, D_MODEL), lambda i, j: (i, 0)),
            pl.BlockSpec((None, 1, D_MODEL), layer_row),
            pl.BlockSpec((None, D_MODEL, CONV_COLS), layer_block),
            pl.BlockSpec((None, CONFORMER_KERNEL, CONV_BLOCK), layer_block),
            pl.BlockSpec((None, 1, CONV_BLOCK), layer_block),
            pl.BlockSpec((None, 1, D_CONV), layer_row),
            pl.BlockSpec((None, 1, D_CONV), layer_row),
        ],
        out_specs=pl.BlockSpec((tile, D_CONV), lambda i, j: (i, 0)),
        scratch_shapes=[
            pltpu.VMEM((tile, D_MODEL), _bf16),
            pltpu.VMEM((N_CONV_BLOCKS, CONV_HALO, CONV_BLOCK), _f32),
            pltpu.VMEM((tile + CONV_HALO, CONV_BLOCK), _f32),
            pltpu.VMEM((SUBLANES - 1, shift_rows, CONV_BLOCK), _f32),
            pltpu.VMEM((N_CONV_BLOCKS, tile, CONV_BLOCK), _f32),
            pltpu.VMEM((N_CONV_BLOCKS, tile, CONV_BLOCK), _f32),
            pltpu.VMEM((tile, CONV_COLS), _f32),
        ],
        compiler_params=pltpu.CompilerParams(
            dimension_semantics=("arbitrary", "arbitrary"), vmem_limit_bytes=VMEM_LIMIT),
        name="conv_branch",
    )(h, nw, w_conv, kw, kb, lnw, lnb)


def _out_kernel(ys_ref, yc_ref, w1_ref, w2_ref, h_ref, o_ref):
    o_ref[...] = h_ref[...] + _dot(ys_ref[...], w1_ref[...]) + _dot(yc_ref[...], w2_ref[...])


def _out_proj(l, ys, yc, wo, h, tm, tn):
    seq = h.shape[0]
    return pl.pallas_call(
        _out_kernel,
        out_shape=jax.ShapeDtypeStruct((seq, D_MODEL), _f32),
        grid=(D_MODEL // tn, seq // tm),
        in_specs=[
            pl.BlockSpec((tm, D_SSM), lambda n, m: (m, 0)),
            pl.BlockSpec((tm, D_CONV), lambda n, m: (m, 0)),
            pl.BlockSpec((None, D_SSM, tn), lambda n, m: (2 * l, 0, n)),
            pl.BlockSpec((None, D_CONV, tn), lambda n, m: (2 * l + 1, 0, n)),
            pl.BlockSpec((tm, tn), lambda n, m: (m, n)),
        ],
        out_specs=pl.BlockSpec((tm, tn), lambda n, m: (m, n)),
        compiler_params=pltpu.CompilerParams(
            dimension_semantics=("arbitrary", "arbitrary"), vmem_limit_bytes=VMEM_LIMIT),
        name="out_proj",
    )(ys, yc, wo, wo, h)


def _final_norm_kernel(h_ref, w_ref, o_ref):
    o_ref[...] = _rmsnorm_rows(h_ref[...], w_ref[...])


def _final_norm(h, w, tile):
    seq = h.shape[0]
    return pl.pallas_call(
        _final_norm_kernel,
        out_shape=jax.ShapeDtypeStruct((seq, D_MODEL), _f32),
        grid=(seq // tile,),
        in_specs=[pl.BlockSpec((tile, D_MODEL), lambda i: (i, 0)),
                  pl.BlockSpec((1, D_MODEL), lambda i: (0, 0))],
        out_specs=pl.BlockSpec((tile, D_MODEL), lambda i: (i, 0)),
        compiler_params=pltpu.CompilerParams(
            dimension_semantics=("arbitrary",), vmem_limit_bytes=VMEM_LIMIT),
        name="final_norm",
    )(h, w)


def _pad_lanes(v):
    return jnp.pad(v, [(0, 0)] * (v.ndim - 1) + [(0, LANES - v.shape[-1])])


def _group_columns(parts, n_groups):
    lead = parts[0].shape[:-1]
    pieces = [p.reshape(lead + (n_groups, p.shape[-1] // n_groups)) for p in parts]
    out = jnp.concatenate(pieces, axis=-1)
    return out.reshape(lead + (n_groups * out.shape[-1],))


def _expansion_matrix():
    head_of_row = jnp.arange(N_SPLIT * LANES) % LANES
    g = jnp.arange(SSM_GROUPS)[:, None, None]
    head_of_col = g * HEADS_PER_GROUP + jnp.arange(GROUP_W)[None, None, :] // SSM_HEAD_DIM
    return (head_of_row[None, :, None] == head_of_col).astype(_bf16)


def _pick_tile(seq, want):
    t = min(seq, want)
    assert seq % t == 0 and t % CHUNK == 0
    return t


def kernel(x, norm_w, w_in, ssm_conv_w, ssm_conv_b, dt_bias, a_log, d_skip, ssm_norm_w, conf_conv_w,
           conf_conv_b, ln_w, ln_b, w_out, final_norm_w):
    batch, seq, _ = x.shape
    assert batch == 1
    tile = _pick_tile(seq, 512)

    w_ssd, w_dt, w_conv = _w_layout(w_in, 512)
    wo = w_out.astype(_bf16).reshape(DEPTH * 2, D_SSM, D_MODEL)

    def xbc_groups(p):
        return _group_columns([p[..., 0:D_SSM], p[..., D_SSM:D_SSM + SSM_GROUPS * D_STATE],
                               p[..., D_SSM + SSM_GROUPS * D_STATE:]], SSM_GROUPS)

    row3 = lambda p: p[:, None, :]
    nw = row3(norm_w)
    cw = xbc_groups(ssm_conv_w)
    cb = row3(xbc_groups(ssm_conv_b))
    dtb = row3(_pad_lanes(dt_bias))
    alog = row3(_pad_lanes(a_log))
    dsk = row3(jnp.repeat(d_skip, SSM_HEAD_DIM, axis=-1))
    gnw = row3(ssm_norm_w)
    kb = row3(conf_conv_b)
    lnw = row3(ln_w)
    lnb = row3(ln_b)
    e_all = _expansion_matrix()

    h = x[0]
    for l in range(DEPTH):
        ys = _ssd_branch(l, h, nw, w_dt, dtb, alog, e_all, w_ssd, cw, cb, dsk, gnw, tile)
        yc = _conv_branch(l, h, nw, w_conv, conf_conv_w, kb, lnw, lnb, tile)
        h = _out_proj(l, ys, yc, wo, h, tile, 1024)
    return _final_norm(h, final_norm_w[None], tile)[None]
```

```python
import jax
import jax.numpy as jnp
from jax import lax
from jax.experimental import pallas as pl
from jax.experimental.pallas import tpu as pltpu

D_MODEL = 2048
DEPTH = 4
CHUNK = 64
D_SSM = 2048
SSM_HEAD_DIM = 64
SSM_HEADS = 32
SSM_GROUPS = 8
HEADS_PER_GROUP = 4
D_STATE = 128
SSM_CONV = 4
D_CONV = 2048
CONFORMER_KERNEL = 31
EPS = 1e-5

GROUP_W = HEADS_PER_GROUP * SSM_HEAD_DIM
SSD_COLS = 2 * GROUP_W + 2 * D_STATE
CONV_BLOCK = 256
N_CONV_BLOCKS = D_CONV // CONV_BLOCK
CONV_COLS = 3 * CONV_BLOCK
LANES = 128
SUBLANES = 8
CONV_HALO = 32
N_SPLIT = 3
VMEM_LIMIT = 48 * 1024 * 1024

OFF_X = D_SSM
OFF_B = OFF_X + D_SSM
OFF_C = OFF_B + SSM_GROUPS * D_STATE
OFF_DT = OFF_C + SSM_GROUPS * D_STATE
OFF_CV = OFF_DT + SSM_HEADS
OFF_CA = OFF_CV + D_CONV
OFF_CG = OFF_CA + D_CONV

_f32 = jnp.float32
_bf16 = jnp.bfloat16


def _dot(a, b):
    return jnp.dot(a, b, preferred_element_type=_f32)


def _dot_nt(a, b):
    return lax.dot_general(a, b, (((1,), (1,)), ((), ())), preferred_element_type=_f32)


def _dot_tn(a, b):
    return lax.dot_general(a, b, (((0,), (0,)), ((), ())), preferred_element_type=_f32)


def _sigmoid(x):
    return 1.0 / (1.0 + jnp.exp(-x))


def _silu(x):
    return x * _sigmoid(x)


def _softplus(x):
    return jnp.maximum(x, 0.0) + jnp.log1p(jnp.exp(-jnp.abs(x)))


def _rmsnorm_rows(x, w):
    ms = jnp.mean(x * x, axis=-1, keepdims=True)
    return x * lax.rsqrt(ms + EPS) * w


def _split_bf16(v):
    pieces = []
    r = v
    for _ in range(N_SPLIT):
        p = r.astype(_bf16)
        pieces.append(p)
        r = r - p.astype(_f32)
    return jnp.concatenate(pieces, axis=-1)


def _chunk_cumsum(v):
    pos = lax.broadcasted_iota(jnp.int32, v.shape, 0) & (CHUNK - 1)
    sh = 1
    while sh < CHUNK:
        shifted = pltpu.roll(v, sh, axis=0)
        v = v + jnp.where(pos >= sh, shifted, 0.0)
        sh *= 2
    return v


def _w_layout_kernel(z_ref, x_ref, b_ref, c_ref, dt_ref, cv_ref, ca_ref, cg_ref,
                     ssd_ref, dtw_ref, conv_ref):
    ssd_ref[...] = jnp.concatenate([z_ref[0], x_ref[0], b_ref[0], c_ref[0]], axis=0).astype(_bf16)
    dtw_ref[...] = dt_ref[0].astype(_bf16)
    conv_ref[...] = jnp.concatenate([cv_ref[0], ca_ref[0], cg_ref[0]], axis=0).astype(_bf16)


def _w_layout(w_t):
    def rows(n, off, step):
        return pl.BlockSpec((pl.Element(1), pl.Element(n), pl.Element(D_MODEL)),
                            lambda l, j: (l, pl.multiple_of(off + step * j, SUBLANES), 0))

    in_specs = [rows(GROUP_W, 0, GROUP_W), rows(GROUP_W, OFF_X, GROUP_W), rows(D_STATE, OFF_B, D_STATE),
                rows(D_STATE, OFF_C, D_STATE), rows(LANES, OFF_DT, 0),
                rows(CONV_BLOCK, OFF_CV, CONV_BLOCK), rows(CONV_BLOCK, OFF_CA, CONV_BLOCK),
                rows(CONV_BLOCK, OFF_CG, CONV_BLOCK)]
    assert SSM_GROUPS == N_CONV_BLOCKS
    return pl.pallas_call(
        _w_layout_kernel,
        out_shape=(jax.ShapeDtypeStruct((DEPTH, SSM_GROUPS * SSD_COLS, D_MODEL), _bf16),
                   jax.ShapeDtypeStruct((DEPTH, LANES, D_MODEL), _bf16),
                   jax.ShapeDtypeStruct((DEPTH, N_CONV_BLOCKS * CONV_COLS, D_MODEL), _bf16)),
        grid=(DEPTH, SSM_GROUPS),
        in_specs=in_specs,
        out_specs=(pl.BlockSpec((None, SSD_COLS, D_MODEL), lambda l, j: (l, j, 0)),
                   pl.BlockSpec((None, LANES, D_MODEL), lambda l, j: (l, 0, 0)),
                   pl.BlockSpec((None, CONV_COLS, D_MODEL), lambda l, j: (l, j, 0))),
        compiler_params=pltpu.CompilerParams(
            dimension_semantics=("arbitrary", "arbitrary"), vmem_limit_bytes=VMEM_LIMIT),
        name="w_layout",
    )(*([w_t] * len(in_specs)))


def _ssd_mix(src, grp, e_ref, cw_ref, cb_ref, dsk_ref, gnw_ref, y_ref, pa_scr, pdt_scr, tail_scr, st_scr, ext_scr):
    tile = src.shape[0]
    n_chunks = tile // CHUNK
    ext_scr[0:SUBLANES, :] = tail_scr[grp]
    ext_scr[SUBLANES:, :] = src[:, GROUP_W:]
    tail_scr[grp] = src[tile - SUBLANES:, GROUP_W:]
    acc = cb_ref[...] + cw_ref[0:1, :] * ext_scr[pl.ds(SUBLANES - SSM_CONV + 1, tile), :]
    for k in range(1, SSM_CONV):
        acc = acc + cw_ref[k:k + 1, :] * ext_scr[pl.ds(SUBLANES - SSM_CONV + 1 + k, tile), :]
    xbc = _silu(acc)
    xs = xbc[:, :GROUP_W]
    bm = xbc[:, GROUP_W:GROUP_W + D_STATE].astype(_bf16)
    cm = xbc[:, GROUP_W + D_STATE:].astype(_bf16)
    a_e = _dot(pa_scr[...], e_ref[...])
    dt_e = _dot(pdt_scr[...], e_ref[...])

    row = lax.broadcasted_iota(jnp.int32, (CHUNK, GROUP_W), 0)
    lane_pos = lax.broadcasted_iota(jnp.int32, (CHUNK, GROUP_W), 1) & (CHUNK - 1)
    diag_mask = row == lane_pos
    causal_mask = row >= lane_pos
    bd_row = lax.broadcasted_iota(jnp.int32, (GROUP_W, GROUP_W), 0) // SSM_HEAD_DIM
    bd_col = lax.broadcasted_iota(jnp.int32, (GROUP_W, GROUP_W), 1) // SSM_HEAD_DIM
    blockdiag_mask = bd_row == bd_col
    dsk = dsk_ref[...]

    state = st_scr[grp]
    ys = []
    for c in range(n_chunks):
        rows = slice(c * CHUNK, (c + 1) * CHUNK)
        x_c, b_c, c_c, a_c = xs[rows], bm[rows], cm[rows], a_e[rows]
        a_last = a_c[CHUNK - 1:CHUNK, :]
        xdt = x_c * dt_e[rows]
        y_off = _dot(c_c, state.astype(_bf16)) * jnp.exp(a_c)
        cb4 = _dot_nt(c_c, jnp.concatenate([b_c] * HEADS_PER_GROUP, axis=0))
        a_row = jnp.sum(jnp.where(diag_mask, a_c, 0.0), axis=0, keepdims=True)
        decay = jnp.where(causal_mask, jnp.exp(a_c - a_row), 0.0)
        scores = (cb4 * decay).astype(_bf16)
        x_bd = jnp.where(blockdiag_mask, jnp.concatenate([xdt] * HEADS_PER_GROUP, axis=0), 0.0).astype(_bf16)
        y_diag = _dot(scores, x_bd)
        xw = (xdt * jnp.exp(a_last - a_c)).astype(_bf16)
        state = state * jnp.exp(a_last) + _dot_tn(b_c, xw)
        ys.append(y_diag + y_off + dsk * x_c)
    st_scr[grp] = state
    yz = jnp.concatenate(ys, axis=0) * _silu(src[:, :GROUP_W])
    y_ref[...] = _rmsnorm_rows(yz, gnw_ref[...]).astype(y_ref.dtype)


def _ssd_kernel(h_ref, nw_ref, wdt_ref, dtb_ref, alog_ref, e_ref, w_ref, cw_ref, cb_ref, dsk_ref, gnw_ref,
                y_ref,
                u_scr, pa_scr, pdt_scr, tail_scr, st_scr, ext_scr, pg_scr):
    i = pl.program_id(0)
    g = pl.program_id(1)

    @pl.when((i == 0) & (g == 0))
    def _():
        tail_scr[...] = jnp.zeros(tail_scr.shape, _f32)
        st_scr[...] = jnp.zeros(st_scr.shape, _f32)

    @pl.when(g == 0)
    def _():
        ub = _rmsnorm_rows(h_ref[...], nw_ref[...]).astype(_bf16)
        u_scr[...] = ub
        dt = _softplus(_dot_nt(ub, wdt_ref[...]) + dtb_ref[...])
        a = -jnp.exp(alog_ref[...])
        pa_scr[...] = _split_bf16(_chunk_cumsum(dt * a))
        pdt_scr[...] = _split_bf16(dt)

    pg_scr[...] = _dot_nt(u_scr[...], w_ref[...])
    _ssd_mix(pg_scr, g, e_ref, cw_ref, cb_ref, dsk_ref, gnw_ref, y_ref, pa_scr, pdt_scr, tail_scr, st_scr, ext_scr)


def _ssd_branch(l, h, nw, wdt, dtb, alog, e_all, w_ssd, cw, cb, dsk, gnw, tile):
    seq = h.shape[0]
    n_tiles = seq // tile
    layer_row = lambda i, g: (l, 0, 0)
    layer_group = lambda i, g: (l, 0, g)
    return pl.pallas_call(
        _ssd_kernel,
        out_shape=jax.ShapeDtypeStruct((seq, D_SSM), _bf16),
        grid=(n_tiles, SSM_GROUPS),
        in_specs=[
            pl.BlockSpec((tile, D_MODEL), lambda i, g: (i, 0)),
            pl.BlockSpec((None, 1, D_MODEL), layer_row),
            pl.BlockSpec((None, LANES, D_MODEL), layer_row),
            pl.BlockSpec((None, 1, LANES), layer_row),
            pl.BlockSpec((None, 1, LANES), layer_row),
            pl.BlockSpec((None, N_SPLIT * LANES, GROUP_W), lambda i, g: (g, 0, 0)),
            pl.BlockSpec((None, SSD_COLS, D_MODEL), lambda i, g: (l, g, 0)),
            pl.BlockSpec((None, SSM_CONV, 2 * GROUP_W), layer_group),
            pl.BlockSpec((None, 1, 2 * GROUP_W), layer_group),
            pl.BlockSpec((None, 1, GROUP_W), layer_group),
            pl.BlockSpec((None, 1, GROUP_W), layer_group),
        ],
        out_specs=pl.BlockSpec((tile, GROUP_W), lambda i, g: (i, g)),
        scratch_shapes=[
            pltpu.VMEM((tile, D_MODEL), _bf16),
            pltpu.VMEM((tile, N_SPLIT * LANES), _bf16),
            pltpu.VMEM((tile, N_SPLIT * LANES), _bf16),
            pltpu.VMEM((SSM_GROUPS, SUBLANES, 2 * GROUP_W), _f32),
            pltpu.VMEM((SSM_GROUPS, D_STATE, GROUP_W), _f32),
            pltpu.VMEM((tile + SUBLANES, 2 * GROUP_W), _f32),
            pltpu.VMEM((tile, SSD_COLS), _f32),
        ],
        compiler_params=pltpu.CompilerParams(
            dimension_semantics=("arbitrary", "arbitrary"), vmem_limit_bytes=VMEM_LIMIT),
        name="ssd_branch",
    )(h, nw, wdt, dtb, alog, e_all, w_ssd, cw, cb, dsk, gnw)


def _conv_mix(src, blk, kw_ref, kb_ref, tail_scr, ext_scr, shift_scr, cc_scr, sg_scr):
    tile = src.shape[0]
    glu = src[:, :CONV_BLOCK] * _sigmoid(src[:, CONV_BLOCK:2 * CONV_BLOCK])
    sg_scr[blk] = _silu(src[:, 2 * CONV_BLOCK:])
    ext_scr[0:CONV_HALO, :] = tail_scr[blk]
    ext_scr[CONV_HALO:, :] = glu
    tail_scr[blk] = glu[tile - CONV_HALO:, :]
    shift_rows = shift_scr.shape[1]
    for sh in range(1, SUBLANES):
        shift_scr[sh - 1] = ext_scr[pl.ds(sh, shift_rows), :]
    first = CONV_HALO - CONFORMER_KERNEL + 1
    acc = kb_ref[...]
    for k in range(CONFORMER_KERNEL):
        major, minor = divmod(first + k, SUBLANES)
        if minor == 0:
            tap = ext_scr[pl.ds(major * SUBLANES, tile), :]
        else:
            tap = shift_scr[minor - 1, pl.ds(major * SUBLANES, tile), :]
        acc = acc + kw_ref[k:k + 1, :] * tap
    cc_scr[blk] = acc


def _conv_layernorm(lnw_ref, lnb_ref, y_ref, cc_scr, sg_scr):
    tile = y_ref.shape[0]
    s1 = jnp.zeros((tile, 1), _f32)
    for jj in range(N_CONV_BLOCKS):
        s1 = s1 + jnp.sum(cc_scr[jj], axis=-1, keepdims=True)
    mu = s1 * (1.0 / D_CONV)
    s2 = jnp.zeros((tile, 1), _f32)
    for jj in range(N_CONV_BLOCKS):
        d = cc_scr[jj] - mu
        s2 = s2 + jnp.sum(d * d, axis=-1, keepdims=True)
    rstd = lax.rsqrt(s2 * (1.0 / D_CONV) + EPS)
    for jj in range(N_CONV_BLOCKS):
        sl = slice(jj * CONV_BLOCK, (jj + 1) * CONV_BLOCK)
        v = (cc_scr[jj] - mu) * rstd * lnw_ref[:, sl] + lnb_ref[:, sl]
        y_ref[:, sl] = (_silu(v) * sg_scr[jj]).astype(y_ref.dtype)


def _conv_kernel(h_ref, nw_ref, w_ref, kw_ref, kb_ref, lnw_ref, lnb_ref,
                 y_ref,
                 u_scr, tail_scr, ext_scr, shift_scr, cc_scr, sg_scr, pc_scr):
    i = pl.program_id(0)
    j = pl.program_id(1)

    @pl.when((i == 0) & (j == 0))
    def _():
        tail_scr[...] = jnp.zeros(tail_scr.shape, _f32)

    @pl.when(j == 0)
    def _():
        u_scr[...] = _rmsnorm_rows(h_ref[...], nw_ref[...]).astype(_bf16)

    pc_scr[...] = _dot_nt(u_scr[...], w_ref[...])
    _conv_mix(pc_scr, j, kw_ref, kb_ref, tail_scr, ext_scr, shift_scr, cc_scr, sg_scr)

    @pl.when(j == N_CONV_BLOCKS - 1)
    def _():
        _conv_layernorm(lnw_ref, lnb_ref, y_ref, cc_scr, sg_scr)


def _conv_branch(l, h, nw, w_conv, kw, kb, lnw, lnb, tile):
    seq = h.shape[0]
    n_tiles = seq // tile
    layer_row = lambda i, j: (l, 0, 0)
    layer_block = lambda i, j: (l, 0, j)
    shift_rows = tile + CONV_HALO - SUBLANES
    return pl.pallas_call(
        _conv_kernel,
        out_shape=jax.ShapeDtypeStruct((seq, D_CONV), _bf16),
        grid=(n_tiles, N_CONV_BLOCKS),
        in_specs=[
            pl.BlockSpec((tile, D_MODEL), lambda i, j: (i, 0)),
            pl.BlockSpec((None, 1, D_MODEL), layer_row),
            pl.BlockSpec((None, CONV_COLS, D_MODEL), lambda i, j: (l, j, 0)),
            pl.BlockSpec((None, CONFORMER_KERNEL, CONV_BLOCK), layer_block),
            pl.BlockSpec((None, 1, CONV_BLOCK), layer_block),
            pl.BlockSpec((None, 1, D_CONV), layer_row),
            pl.BlockSpec((None, 1, D_CONV), layer_row),
        ],
        out_specs=pl.BlockSpec((tile, D_CONV), lambda i, j: (i, 0)),
        scratch_shapes=[
            pltpu.VMEM((tile, D_MODEL), _bf16),
            pltpu.VMEM((N_CONV_BLOCKS, CONV_HALO, CONV_BLOCK), _f32),
            pltpu.VMEM((tile + CONV_HALO, CONV_BLOCK), _f32),
            pltpu.VMEM((SUBLANES - 1, shift_rows, CONV_BLOCK), _f32),
            pltpu.VMEM((N_CONV_BLOCKS, tile, CONV_BLOCK), _f32),
            pltpu.VMEM((N_CONV_BLOCKS, tile, CONV_BLOCK), _f32),
            pltpu.VMEM((tile, CONV_COLS), _f32),
        ],
        compiler_params=pltpu.CompilerParams(
            dimension_semantics=("arbitrary", "arbitrary"), vmem_limit_bytes=VMEM_LIMIT),
        name="conv_branch",
    )(h, nw, w_conv, kw, kb, lnw, lnb)


def _out_kernel(ys_ref, yc_ref, w1_ref, w2_ref, h_ref, o_ref):
    o_ref[...] = h_ref[...] + _dot(ys_ref[...], w1_ref[...]) + _dot(yc_ref[...], w2_ref[...])


def _out_proj(l, ys, yc, wo, h, tm, tn):
    seq = h.shape[0]
    return pl.pallas_call(
        _out_kernel,
        out_shape=jax.ShapeDtypeStruct((seq, D_MODEL), _f32),
        grid=(D_MODEL // tn, seq // tm),
        in_specs=[
            pl.BlockSpec((tm, D_SSM), lambda n, m: (m, 0)),
            pl.BlockSpec((tm, D_CONV), lambda n, m: (m, 0)),
            pl.BlockSpec((None, D_SSM, tn), lambda n, m: (2 * l, 0, n)),
            pl.BlockSpec((None, D_CONV, tn), lambda n, m: (2 * l + 1, 0, n)),
            pl.BlockSpec((tm, tn), lambda n, m: (m, n)),
        ],
        out_specs=pl.BlockSpec((tm, tn), lambda n, m: (m, n)),
        compiler_params=pltpu.CompilerParams(
            dimension_semantics=("arbitrary", "arbitrary"), vmem_limit_bytes=VMEM_LIMIT),
        name="out_proj",
    )(ys, yc, wo, wo, h)


def _final_norm_kernel(h_ref, w_ref, o_ref):
    o_ref[...] = _rmsnorm_rows(h_ref[...], w_ref[...])


def _final_norm(h, w, tile):
    seq = h.shape[0]
    return pl.pallas_call(
        _final_norm_kernel,
        out_shape=jax.ShapeDtypeStruct((seq, D_MODEL), _f32),
        grid=(seq // tile,),
        in_specs=[pl.BlockSpec((tile, D_MODEL), lambda i: (i, 0)),
                  pl.BlockSpec((1, D_MODEL), lambda i: (0, 0))],
        out_specs=pl.BlockSpec((tile, D_MODEL), lambda i: (i, 0)),
        compiler_params=pltpu.CompilerParams(
            dimension_semantics=("arbitrary",), vmem_limit_bytes=VMEM_LIMIT),
        name="final_norm",
    )(h, w)


def _pad_lanes(v):
    return jnp.pad(v, [(0, 0)] * (v.ndim - 1) + [(0, LANES - v.shape[-1])])


def _group_columns(parts, n_groups):
    lead = parts[0].shape[:-1]
    pieces = [p.reshape(lead + (n_groups, p.shape[-1] // n_groups)) for p in parts]
    out = jnp.concatenate(pieces, axis=-1)
    return out.reshape(lead + (n_groups * out.shape[-1],))


def _expansion_matrix():
    head_of_row = jnp.arange(N_SPLIT * LANES) % LANES
    g = jnp.arange(SSM_GROUPS)[:, None, None]
    head_of_col = g * HEADS_PER_GROUP + jnp.arange(GROUP_W)[None, None, :] // SSM_HEAD_DIM
    return (head_of_row[None, :, None] == head_of_col).astype(_bf16)


def _pick_tile(seq, want):
    t = min(seq, want)
    assert seq % t == 0 and t % CHUNK == 0
    return t


def kernel(x, norm_w, w_in, ssm_conv_w, ssm_conv_b, dt_bias, a_log, d_skip, ssm_norm_w, conf_conv_w,
           conf_conv_b, ln_w, ln_b, w_out, final_norm_w):
    batch, seq, _ = x.shape
    assert batch == 1
    tile = _pick_tile(seq, 512)

    w_ssd, w_dt, w_conv = _w_layout(jnp.swapaxes(w_in, 1, 2))
    wo = w_out.astype(_bf16).reshape(DEPTH * 2, D_SSM, D_MODEL)

    def xbc_groups(p):
        return _group_columns([p[..., 0:D_SSM], p[..., D_SSM:D_SSM + SSM_GROUPS * D_STATE],
                               p[..., D_SSM + SSM_GROUPS * D_STATE:]], SSM_GROUPS)

    row3 = lambda p: p[:, None, :]
    nw = row3(norm_w)
    cw = xbc_groups(ssm_conv_w)
    cb = row3(xbc_groups(ssm_conv_b))
    dtb = row3(_pad_lanes(dt_bias))
    alog = row3(_pad_lanes(a_log))
    dsk = row3(jnp.repeat(d_skip, SSM_HEAD_DIM, axis=-1))
    gnw = row3(ssm_norm_w)
    kb = row3(conf_conv_b)
    lnw = row3(ln_w)
    lnb = row3(ln_b)
    e_all = _expansion_matrix()

    h = x[0]
    for l in range(DEPTH):
        ys = _ssd_branch(l, h, nw, w_dt, dtb, alog, e_all, w_ssd, cw, cb, dsk, gnw, tile)
        yc = _conv_branch(l, h, nw, w_conv, conf_conv_w, kb, lnw, lnb, tile)
        h = _out_proj(l, ys, yc, wo, h, tile, 1024)
    return _final_norm(h, final_norm_w[None], tile)[None]
```

```python
import jax
import jax.numpy as jnp
from jax import lax
from jax.experimental import pallas as pl
from jax.experimental.pallas import tpu as pltpu

D_MODEL = 2048
DEPTH = 4
CHUNK = 64
D_SSM = 2048
SSM_HEAD_DIM = 64
SSM_HEADS = 32
SSM_GROUPS = 8
HEADS_PER_GROUP = 4
D_STATE = 128
SSM_CONV = 4
D_CONV = 2048
CONFORMER_KERNEL = 31
EPS = 1e-5

GROUP_W = HEADS_PER_GROUP * SSM_HEAD_DIM
SSD_COLS = 2 * GROUP_W + 2 * D_STATE
CONV_BLOCK = 256
N_CONV_BLOCKS = D_CONV // CONV_BLOCK
CONV_COLS = 3 * CONV_BLOCK
LANES = 128
SUBLANES = 8
CONV_HALO = 32
N_SPLIT = 3
VMEM_LIMIT = 48 * 1024 * 1024

OFF_X = D_SSM
OFF_B = OFF_X + D_SSM
OFF_C = OFF_B + SSM_GROUPS * D_STATE
OFF_DT = OFF_C + SSM_GROUPS * D_STATE
OFF_CV = OFF_DT + SSM_HEADS
OFF_CA = OFF_CV + D_CONV
OFF_CG = OFF_CA + D_CONV

_f32 = jnp.float32
_bf16 = jnp.bfloat16


def _dot(a, b):
    return jnp.dot(a, b, preferred_element_type=_f32)


def _dot_nt(a, b):
    return lax.dot_general(a, b, (((1,), (1,)), ((), ())), preferred_element_type=_f32)


def _dot_tn(a, b):
    return lax.dot_general(a, b, (((0,), (0,)), ((), ())), preferred_element_type=_f32)


def _sigmoid(x):
    return 1.0 / (1.0 + jnp.exp(-x))


def _silu(x):
    return x * _sigmoid(x)


def _softplus(x):
    return jnp.maximum(x, 0.0) + jnp.log1p(jnp.exp(-jnp.abs(x)))


def _rmsnorm_rows(x, w):
    ms = jnp.mean(x * x, axis=-1, keepdims=True)
    return x * lax.rsqrt(ms + EPS) * w


def _split_bf16(v):
    pieces = []
    r = v
    for _ in range(N_SPLIT):
        p = r.astype(_bf16)
        pieces.append(p)
        r = r - p.astype(_f32)
    return jnp.concatenate(pieces, axis=-1)


def _chunk_cumsum(v):
    pos = lax.broadcasted_iota(jnp.int32, v.shape, 0) & (CHUNK - 1)
    sh = 1
    while sh < CHUNK:
        shifted = pltpu.roll(v, sh, axis=0)
        v = v + jnp.where(pos >= sh, shifted, 0.0)
        sh *= 2
    return v


def _w_layout_kernel(z_ref, x_ref, b_ref, c_ref, dt_ref, cv_ref, ca_ref, cg_ref,
                     ssd_ref, dtw_ref, conv_ref):
    ssd_ref[...] = jnp.concatenate([z_ref[0].T, x_ref[0].T, b_ref[0].T, c_ref[0].T], axis=1).astype(_bf16)
    dtw_ref[...] = dt_ref[0].T.astype(_bf16)
    conv_ref[...] = jnp.concatenate([cv_ref[0].T, ca_ref[0].T, cg_ref[0].T], axis=1).astype(_bf16)


def _w_layout(w_t):
    def rows(n, off, step):
        return pl.BlockSpec((pl.Element(1), pl.Element(n), pl.Element(D_MODEL)),
                            lambda l, j: (l, pl.multiple_of(off + step * j, SUBLANES), 0))

    in_specs = [rows(GROUP_W, 0, GROUP_W), rows(GROUP_W, OFF_X, GROUP_W), rows(D_STATE, OFF_B, D_STATE),
                rows(D_STATE, OFF_C, D_STATE), rows(LANES, OFF_DT, 0),
                rows(CONV_BLOCK, OFF_CV, CONV_BLOCK), rows(CONV_BLOCK, OFF_CA, CONV_BLOCK),
                rows(CONV_BLOCK, OFF_CG, CONV_BLOCK)]
    assert SSM_GROUPS == N_CONV_BLOCKS
    return pl.pallas_call(
        _w_layout_kernel,
        out_shape=(jax.ShapeDtypeStruct((DEPTH, D_MODEL, SSM_GROUPS * SSD_COLS), _bf16),
                   jax.ShapeDtypeStruct((DEPTH, D_MODEL, LANES), _bf16),
                   jax.ShapeDtypeStruct((DEPTH, D_MODEL, N_CONV_BLOCKS * CONV_COLS), _bf16)),
        grid=(DEPTH, SSM_GROUPS),
        in_specs=in_specs,
        out_specs=(pl.BlockSpec((None, D_MODEL, SSD_COLS), lambda l, j: (l, 0, j)),
                   pl.BlockSpec((None, D_MODEL, LANES), lambda l, j: (l, 0, 0)),
                   pl.BlockSpec((None, D_MODEL, CONV_COLS), lambda l, j: (l, 0, j))),
        compiler_params=pltpu.CompilerParams(
            dimension_semantics=("arbitrary", "arbitrary"), vmem_limit_bytes=VMEM_LIMIT),
        name="w_layout",
    )(*([w_t] * len(in_specs)))


def _ssd_mix(src, grp, e_ref, cw_ref, cb_ref, dsk_ref, gnw_ref, y_ref, pa_scr, pdt_scr, tail_scr, st_scr, ext_scr):
    tile = src.shape[0]
    n_chunks = tile // CHUNK
    ext_scr[0:SUBLANES, :] = tail_scr[grp]
    ext_scr[SUBLANES:, :] = src[:, GROUP_W:]
    tail_scr[grp] = src[tile - SUBLANES:, GROUP_W:]
    acc = cb_ref[...] + cw_ref[0:1, :] * ext_scr[pl.ds(SUBLANES - SSM_CONV + 1, tile), :]
    for k in range(1, SSM_CONV):
        acc = acc + cw_ref[k:k + 1, :] * ext_scr[pl.ds(SUBLANES - SSM_CONV + 1 + k, tile), :]
    xbc = _silu(acc)
    xs = xbc[:, :GROUP_W]
    bm = xbc[:, GROUP_W:GROUP_W + D_STATE].astype(_bf16)
    cm = xbc[:, GROUP_W + D_STATE:].astype(_bf16)
    a_e = _dot(pa_scr[...], e_ref[...])
    dt_e = _dot(pdt_scr[...], e_ref[...])

    row = lax.broadcasted_iota(jnp.int32, (CHUNK, GROUP_W), 0)
    lane_pos = lax.broadcasted_iota(jnp.int32, (CHUNK, GROUP_W), 1) & (CHUNK - 1)
    diag_mask = row == lane_pos
    causal_mask = row >= lane_pos
    bd_row = lax.broadcasted_iota(jnp.int32, (GROUP_W, GROUP_W), 0) // SSM_HEAD_DIM
    bd_col = lax.broadcasted_iota(jnp.int32, (GROUP_W, GROUP_W), 1) // SSM_HEAD_DIM
    blockdiag_mask = bd_row == bd_col
    dsk = dsk_ref[...]

    state = st_scr[grp]
    ys = []
    for c in range(n_chunks):
        rows = slice(c * CHUNK, (c + 1) * CHUNK)
        x_c, b_c, c_c, a_c = xs[rows], bm[rows], cm[rows], a_e[rows]
        a_last = a_c[CHUNK - 1:CHUNK, :]
        xdt = x_c * dt_e[rows]
        y_off = _dot(c_c, state.astype(_bf16)) * jnp.exp(a_c)
        cb4 = _dot_nt(c_c, jnp.concatenate([b_c] * HEADS_PER_GROUP, axis=0))
        a_row = jnp.sum(jnp.where(diag_mask, a_c, 0.0), axis=0, keepdims=True)
        decay = jnp.where(causal_mask, jnp.exp(a_c - a_row), 0.0)
        scores = (cb4 * decay).astype(_bf16)
        x_bd = jnp.where(blockdiag_mask, jnp.concatenate([xdt] * HEADS_PER_GROUP, axis=0), 0.0).astype(_bf16)
        y_diag = _dot(scores, x_bd)
        xw = (xdt * jnp.exp(a_last - a_c)).astype(_bf16)
        state = state * jnp.exp(a_last) + _dot_tn(b_c, xw)
        ys.append(y_diag + y_off + dsk * x_c)
    st_scr[grp] = state
    yz = jnp.concatenate(ys, axis=0) * _silu(src[:, :GROUP_W])
    y_ref[...] = _rmsnorm_rows(yz, gnw_ref[...]).astype(y_ref.dtype)


def _ssd_kernel(h_ref, nw_ref, wdt_ref, dtb_ref, alog_ref, e_ref, w_ref, cw_ref, cb_ref, dsk_ref, gnw_ref,
                y_ref,
                u_scr, pa_scr, pdt_scr, tail_scr, st_scr, ext_scr, pg_scr):
    i = pl.program_id(0)
    g = pl.program_id(1)

    @pl.when((i == 0) & (g == 0))
    def _():
        tail_scr[...] = jnp.zeros(tail_scr.shape, _f32)
        st_scr[...] = jnp.zeros(st_scr.shape, _f32)

    @pl.when(g == 0)
    def _():
        ub = _rmsnorm_rows(h_ref[...], nw_ref[...]).astype(_bf16)
        u_scr[...] = ub
        dt = _softplus(_dot(ub, wdt_ref[...]) + dtb_ref[...])
        a = -jnp.exp(alog_ref[...])
        pa_scr[...] = _split_bf16(_chunk_cumsum(dt * a))
        pdt_scr[...] = _split_bf16(dt)

    pg_scr[...] = _dot(u_scr[...], w_ref[...])
    _ssd_mix(pg_scr, g, e_ref, cw_ref, cb_ref, dsk_ref, gnw_ref, y_ref, pa_scr, pdt_scr, tail_scr, st_scr, ext_scr)


def _ssd_branch(l, h, nw, wdt, dtb, alog, e_all, w_ssd, cw, cb, dsk, gnw, tile):
    seq = h.shape[0]
    n_tiles = seq // tile
    layer_row = lambda i, g: (l, 0, 0)
    layer_group = lambda i, g: (l, 0, g)
    return pl.pallas_call(
        _ssd_kernel,
        out_shape=jax.ShapeDtypeStruct((seq, D_SSM), _bf16),
        grid=(n_tiles, SSM_GROUPS),
        in_specs=[
            pl.BlockSpec((tile, D_MODEL), lambda i, g: (i, 0)),
            pl.BlockSpec((None, 1, D_MODEL), layer_row),
            pl.BlockSpec((None, D_MODEL, LANES), layer_row),
            pl.BlockSpec((None, 1, LANES), layer_row),
            pl.BlockSpec((None, 1, LANES), layer_row),
            pl.BlockSpec((None, N_SPLIT * LANES, GROUP_W), lambda i, g: (g, 0, 0)),
            pl.BlockSpec((None, D_MODEL, SSD_COLS), layer_group),
            pl.BlockSpec((None, SSM_CONV, 2 * GROUP_W), layer_group),
            pl.BlockSpec((None, 1, 2 * GROUP_W), layer_group),
            pl.BlockSpec((None, 1, GROUP_W), layer_group),
            pl.BlockSpec((None, 1, GROUP_W), layer_group),
        ],
        out_specs=pl.BlockSpec((tile, GROUP_W), lambda i, g: (i, g)),
        scratch_shapes=[
            pltpu.VMEM((tile, D_MODEL), _bf16),
            pltpu.VMEM((tile, N_SPLIT * LANES), _bf16),
            pltpu.VMEM((tile, N_SPLIT * LANES), _bf16),
            pltpu.VMEM((SSM_GROUPS, SUBLANES, 2 * GROUP_W), _f32),
            pltpu.VMEM((SSM_GROUPS, D_STATE, GROUP_W), _f32),
            pltpu.VMEM((tile + SUBLANES, 2 * GROUP_W), _f32),
            pltpu.VMEM((tile, SSD_COLS), _f32),
        ],
        compiler_params=pltpu.CompilerParams(
            dimension_semantics=("arbitrary", "arbitrary"), vmem_limit_bytes=VMEM_LIMIT),
        name="ssd_branch",
    )(h, nw, wdt, dtb, alog, e_all, w_ssd, cw, cb, dsk, gnw)


def _conv_mix(src, blk, kw_ref, kb_ref, tail_scr, ext_scr, shift_scr, cc_scr, sg_scr):
    tile = src.shape[0]
    glu = src[:, :CONV_BLOCK] * _sigmoid(src[:, CONV_BLOCK:2 * CONV_BLOCK])
    sg_scr[blk] = _silu(src[:, 2 * CONV_BLOCK:])
    ext_scr[0:CONV_HALO, :] = tail_scr[blk]
    ext_scr[CONV_HALO:, :] = glu
    tail_scr[blk] = glu[tile - CONV_HALO:, :]
    shift_rows = shift_scr.shape[1]
    for sh in range(1, SUBLANES):
        shift_scr[sh - 1] = ext_scr[pl.ds(sh, shift_rows), :]
    first = CONV_HALO - CONFORMER_KERNEL + 1
    acc = kb_ref[...]
    for k in range(CONFORMER_KERNEL):
        major, minor = divmod(first + k, SUBLANES)
        if minor == 0:
            tap = ext_scr[pl.ds(major * SUBLANES, tile), :]
        else:
            tap = shift_scr[minor - 1, pl.ds(major * SUBLANES, tile), :]
        acc = acc + kw_ref[k:k + 1, :] * tap
    cc_scr[blk] = acc


def _conv_layernorm(lnw_ref, lnb_ref, y_ref, cc_scr, sg_scr):
    tile = y_ref.shape[0]
    s1 = jnp.zeros((tile, 1), _f32)
    for jj in range(N_CONV_BLOCKS):
        s1 = s1 + jnp.sum(cc_scr[jj], axis=-1, keepdims=True)
    mu = s1 * (1.0 / D_CONV)
    s2 = jnp.zeros((tile, 1), _f32)
    for jj in range(N_CONV_BLOCKS):
        d = cc_scr[jj] - mu
        s2 = s2 + jnp.sum(d * d, axis=-1, keepdims=True)
    rstd = lax.rsqrt(s2 * (1.0 / D_CONV) + EPS)
    for jj in range(N_CONV_BLOCKS):
        sl = slice(jj * CONV_BLOCK, (jj + 1) * CONV_BLOCK)
        v = (cc_scr[jj] - mu) * rstd * lnw_ref[:, sl] + lnb_ref[:, sl]
        y_ref[:, sl] = (_silu(v) * sg_scr[jj]).astype(y_ref.dtype)


def _conv_kernel(h_ref, nw_ref, w_ref, kw_ref, kb_ref, lnw_ref, lnb_ref,
                 y_ref,
                 u_scr, tail_scr, ext_scr, shift_scr, cc_scr, sg_scr, pc_scr):
    i = pl.program_id(0)
    j = pl.program_id(1)

    @pl.when((i == 0) & (j == 0))
    def _():
        tail_scr[...] = jnp.zeros(tail_scr.shape, _f32)

    @pl.when(j == 0)
    def _():
        u_scr[...] = _rmsnorm_rows(h_ref[...], nw_ref[...]).astype(_bf16)

    pc_scr[...] = _dot(u_scr[...], w_ref[...])
    _conv_mix(pc_scr, j, kw_ref, kb_ref, tail_scr, ext_scr, shift_scr, cc_scr, sg_scr)

    @pl.when(j == N_CONV_BLOCKS - 1)
    def _():
        _conv_layernorm(lnw_ref, lnb_ref, y_ref, cc_scr, sg_scr)


def _conv_branch(l, h, nw, w_conv, kw, kb, lnw, lnb, tile):
    seq = h.shape[0]
    n_tiles = seq // tile
    layer_row = lambda i, j: (l, 0, 0)
    layer_block = lambda i, j: (l, 0, j)
    shift_rows = tile + CONV_HALO - SUBLANES
    return pl.pallas_call(
        _conv_kernel,
        out_shape=jax.ShapeDtypeStruct((seq, D_CONV), _bf16),
        grid=(n_tiles, N_CONV_BLOCKS),
        in_specs=[
            pl.BlockSpec((tile, D_MODEL), lambda i, j: (i, 0)),
            pl.BlockSpec((None, 1, D_MODEL), layer_row),
            pl.BlockSpec((None, D_MODEL, CONV_COLS), layer_block),
            pl.BlockSpec((None, CONFORMER_KERNEL, CONV_BLOCK), layer_block),
            pl.BlockSpec((None, 1, CONV_BLOCK), layer_block),
            pl.BlockSpec((None, 1, D_CONV), layer_row),
            pl.BlockSpec((None, 1, D_CONV), layer_row),
        ],
        out_specs=pl.BlockSpec((tile, D_CONV), lambda i, j: (i, 0)),
        scratch_shapes=[
            pltpu.VMEM((tile, D_MODEL), _bf16),
            pltpu.VMEM((N_CONV_BLOCKS, CONV_HALO, CONV_BLOCK), _f32),
            pltpu.VMEM((tile + CONV_HALO, CONV_BLOCK), _f32),
            pltpu.VMEM((SUBLANES - 1, shift_rows, CONV_BLOCK), _f32),
            pltpu.VMEM((N_CONV_BLOCKS, tile, CONV_BLOCK), _f32),
            pltpu.VMEM((N_CONV_BLOCKS, tile, CONV_BLOCK), _f32),
            pltpu.VMEM((tile, CONV_COLS), _f32),
        ],
        compiler_params=pltpu.CompilerParams(
            dimension_semantics=("arbitrary", "arbitrary"), vmem_limit_bytes=VMEM_LIMIT),
        name="conv_branch",
    )(h, nw, w_conv, kw, kb, lnw, lnb)


def _out_kernel(ys_ref, yc_ref, w1_ref, w2_ref, h_ref, o_ref):
    o_ref[...] = h_ref[...] + _dot(ys_ref[...], w1_ref[...]) + _dot(yc_ref[...], w2_ref[...])


def _out_proj(l, ys, yc, wo, h, tm, tn):
    seq = h.shape[0]
    return pl.pallas_call(
        _out_kernel,
        out_shape=jax.ShapeDtypeStruct((seq, D_MODEL), _f32),
        grid=(D_MODEL // tn, seq // tm),
        in_specs=[
            pl.BlockSpec((tm, D_SSM), lambda n, m: (m, 0)),
            pl.BlockSpec((tm, D_CONV), lambda n, m: (m, 0)),
            pl.BlockSpec((None, D_SSM, tn), lambda n, m: (2 * l, 0, n)),
            pl.BlockSpec((None, D_CONV, tn), lambda n, m: (2 * l + 1, 0, n)),
            pl.BlockSpec((tm, tn), lambda n, m: (m, n)),
        ],
        out_specs=pl.BlockSpec((tm, tn), lambda n, m: (m, n)),
        compiler_params=pltpu.CompilerParams(
            dimension_semantics=("arbitrary", "arbitrary"), vmem_limit_bytes=VMEM_LIMIT),
        name="out_proj",
    )(ys, yc, wo, wo, h)


def _final_norm_kernel(h_ref, w_ref, o_ref):
    o_ref[...] = _rmsnorm_rows(h_ref[...], w_ref[...])


def _final_norm(h, w, tile):
    seq = h.shape[0]
    return pl.pallas_call(
        _final_norm_kernel,
        out_shape=jax.ShapeDtypeStruct((seq, D_MODEL), _f32),
        grid=(seq // tile,),
        in_specs=[pl.BlockSpec((tile, D_MODEL), lambda i: (i, 0)),
                  pl.BlockSpec((1, D_MODEL), lambda i: (0, 0))],
        out_specs=pl.BlockSpec((tile, D_MODEL), lambda i: (i, 0)),
        compiler_params=pltpu.CompilerParams(
            dimension_semantics=("arbitrary",), vmem_limit_bytes=VMEM_LIMIT),
        name="final_norm",
    )(h, w)


def _pad_lanes(v):
    return jnp.pad(v, [(0, 0)] * (v.ndim - 1) + [(0, LANES - v.shape[-1])])


def _group_columns(parts, n_groups):
    lead = parts[0].shape[:-1]
    pieces = [p.reshape(lead + (n_groups, p.shape[-1] // n_groups)) for p in parts]
    out = jnp.concatenate(pieces, axis=-1)
    return out.reshape(lead + (n_groups * out.shape[-1],))


def _expansion_matrix():
    head_of_row = jnp.arange(N_SPLIT * LANES) % LANES
    g = jnp.arange(SSM_GROUPS)[:, None, None]
    head_of_col = g * HEADS_PER_GROUP + jnp.arange(GROUP_W)[None, None, :] // SSM_HEAD_DIM
    return (head_of_row[None, :, None] == head_of_col).astype(_bf16)


def _pick_tile(seq, want):
    t = min(seq, want)
    assert seq % t == 0 and t % CHUNK == 0
    return t


def kernel(x, norm_w, w_in, ssm_conv_w, ssm_conv_b, dt_bias, a_log, d_skip, ssm_norm_w, conf_conv_w,
           conf_conv_b, ln_w, ln_b, w_out, final_norm_w):
    batch, seq, _ = x.shape
    assert batch == 1
    tile = _pick_tile(seq, 512)
    ssd_tile = _pick_tile(seq, 1024)

    w_ssd, w_dt, w_conv = _w_layout(jnp.swapaxes(w_in, 1, 2))
    wo = w_out.astype(_bf16).reshape(DEPTH * 2, D_SSM, D_MODEL)

    def xbc_groups(p):
        return _group_columns([p[..., 0:D_SSM], p[..., D_SSM:D_SSM + SSM_GROUPS * D_STATE],
                               p[..., D_SSM + SSM_GROUPS * D_STATE:]], SSM_GROUPS)

    row3 = lambda p: p[:, None, :]
    nw = row3(norm_w)
    cw = xbc_groups(ssm_conv_w)
    cb = row3(xbc_groups(ssm_conv_b))
    dtb = row3(_pad_lanes(dt_bias))
    alog = row3(_pad_lanes(a_log))
    dsk = row3(jnp.repeat(d_skip, SSM_HEAD_DIM, axis=-1))
    gnw = row3(ssm_norm_w)
    kb = row3(conf_conv_b)
    lnw = row3(ln_w)
    lnb = row3(ln_b)
    e_all = _expansion_matrix()

    h = x[0]
    for l in range(DEPTH):
        ys = _ssd_branch(l, h, nw, w_dt, dtb, alog, e_all, w_ssd, cw, cb, dsk, gnw, ssd_tile)
        yc = _conv_branch(l, h, nw, w_conv, conf_conv_w, kb, lnw, lnb, tile)
        h = _out_proj(l, ys, yc, wo, h, tile, 1024)
    return _final_norm(h, final_norm_w[None], tile)[None]
```
